```python
import jax, jax.numpy as jnp
from jax import lax
import numpy as np

D_MODEL = 1024
BATCH = 2
SEQ = 8192
DEPTH = 2
DEC_BATCH = 32
DEC_SEQ = 8
PAST_LEN = 16384
PAGE_SIZE = 128

N_MIXERS = 2
N_CONV_LAYERS = (DEPTH + 1) // 2
N_SB_LAYERS = DEPTH // 2
CONV_CH = D_MODEL
CONV_WIDTH = 31
HEAD_DIM = 128
SB_HEADS = 8
SB_WIDTH = SB_HEADS * HEAD_DIM
MEM_LEN = 256
MEM_HEADS = 4
MEM_WIDTH = MEM_HEADS * HEAD_DIM
MIX_OUT = D_MODEL + MEM_WIDTH
D_FF = 2816
FFN_CONV_WIDTH = 3
Q_BLOCK = 128
ATTN_SCALE = HEAD_DIM ** -0.5
RMS_EPS = 1e-6
LN_EPS = 1e-5
SB_BIAS_HI = -4.0
SB_BIAS_LO = -10.0

kernel_name = "conformer_conv_stickbreaking_memxattn_convffn_step"


def _rmsnorm(x, g):
    xf = x.astype(jnp.float32)
    y = xf * lax.rsqrt(jnp.mean(xf * xf, axis=-1, keepdims=True) + RMS_EPS)
    return (y * g.astype(jnp.float32)).astype(x.dtype)


def _layernorm(x, g, b):
    xf = x.astype(jnp.float32)
    mu = jnp.mean(xf, axis=-1, keepdims=True)
    xc = xf - mu
    y = xc * lax.rsqrt(jnp.mean(xc * xc, axis=-1, keepdims=True) + LN_EPS)
    return (y * g.astype(jnp.float32) + b.astype(jnp.float32)).astype(x.dtype)


def _causal_dwconv(x_ext, w, b):
    c = x_ext.shape[-1]
    y = lax.conv_general_dilated(x_ext, w[:, None, :].astype(x_ext.dtype), (1,), 'VALID',
                                 dimension_numbers=('NWC', 'WIO', 'NWC'),
                                 feature_group_count=c)
    return y + b.astype(y.dtype)


def _sb_block(q, ks, vs, q_pos, k_pos, bias):
    z = jnp.concatenate([jnp.einsum('bqhd,bkhd->bhqk', q, kk) for kk in ks],
                        axis=-1).astype(jnp.float32) * ATTN_SCALE
    z = z + bias.astype(jnp.float32)[None, :, None, None]
    mask = k_pos[None, :] < q_pos[:, None]
    log_beta = jax.nn.log_sigmoid(z)
    log_keep = jnp.where(mask, log_beta - z, 0.0)
    after = lax.cumsum(log_keep, axis=3, reverse=True) - log_keep
    a = jnp.where(mask, jnp.exp(log_beta + after), 0.0).astype(q.dtype)
    out = None
    off = 0
    for kk, vv in zip(ks, vs):
        n = kk.shape[1]
        part = jnp.einsum('bhqk,bkhd->bqhd', a[..., off:off + n], vv)
        out = part if out is None else out + part
        off += n
    return out


def _sb_attention(q, ks, vs, q_start, bias):
    b, tq, h, d = q.shape
    tk = sum(kk.shape[1] for kk in ks)
    k_pos = jnp.arange(tk, dtype=jnp.int32)
    q_pos = q_start + jnp.arange(tq, dtype=jnp.int32)
    nb = tq // Q_BLOCK
    if tq % Q_BLOCK == 0 and nb > 1:
        qb = q.reshape(b, nb, Q_BLOCK, h, d).swapaxes(0, 1)
        pb = q_pos.reshape(nb, Q_BLOCK)
        ob = lax.map(lambda a: _sb_block(a[0], ks, vs, a[1], k_pos, bias), (qb, pb))
        return ob.swapaxes(0, 1).reshape(b, tq, h, d)
    return _sb_block(q, ks, vs, q_pos, k_pos, bias)


def _mem_attention(q, mk, mv):
    s = jnp.einsum('bqhd,bmhd->bhqm', q, mk).astype(jnp.float32) * ATTN_SCALE
    p = jax.nn.softmax(s, axis=-1).astype(mv.dtype)
    return jnp.einsum('bhqm,bmhd->bqhd', p, mv)


def setup_inputs(seed: int = 0) -> dict:
    key = jax.random.key(seed)
    k = jax.random.split(key, 32)
    f32 = jnp.float32
    n_pages = PAST_LEN // PAGE_SIZE
    n_pool = (5 * DEC_BATCH * n_pages) // 4
    page_table = jax.random.permutation(k[9], n_pool)[:DEC_BATCH * n_pages]
    page_table = page_table.reshape(DEC_BATCH, n_pages).astype(jnp.int32)
    nrm = lambda i, shape, scale: jax.random.normal(k[i], shape, f32) * scale
    sb_bias = (jnp.linspace(SB_BIAS_HI, SB_BIAS_LO, SB_HEADS, dtype=f32)[None, :]
               + nrm(26, (N_SB_LAYERS, SB_HEADS), 0.1))
    return {
        "x_prompt": nrm(0, (BATCH, SEQ, D_MODEL), 1.0),
        "x_sample": nrm(1, (DEC_BATCH, DEC_SEQ, D_MODEL), 1.0),
        "cache_conv": nrm(2, (N_CONV_LAYERS, DEC_BATCH, CONV_WIDTH - 1, CONV_CH), 0.5),
        "cache_sb_k": nrm(3, (n_pool, N_SB_LAYERS, PAGE_SIZE, SB_HEADS, HEAD_DIM), 1.0),
        "cache_sb_v": nrm(4, (n_pool, N_SB_LAYERS, PAGE_SIZE, SB_HEADS, HEAD_DIM), 1.0),
        "cache_mem_k": nrm(5, (DEPTH, DEC_BATCH, MEM_LEN, MEM_HEADS, HEAD_DIM), 1.0),
        "cache_mem_v": nrm(6, (DEPTH, DEC_BATCH, MEM_LEN, MEM_HEADS, HEAD_DIM), 1.0),
        "state_ffn_conv": nrm(7, (DEPTH, DEC_BATCH, FFN_CONV_WIDTH - 1, 2 * D_FF), 1.0),
        "page_table": page_table,
        "mem_prompt": nrm(8, (BATCH, MEM_LEN, D_MODEL), 1.0),
        "norm_mix_g": 1.0 + nrm(10, (DEPTH, D_MODEL), 0.02),
        "norm_ffn_g": 1.0 + nrm(11, (DEPTH, D_MODEL), 0.02),
        "norm_final_g": 1.0 + nrm(12, (D_MODEL,), 0.02),
        "conv_w_in": nrm(13, (N_CONV_LAYERS, D_MODEL, 2 * CONV_CH + MEM_WIDTH), D_MODEL ** -0.5),
        "conv_dw_w": nrm(14, (N_CONV_LAYERS, CONV_WIDTH, CONV_CH), CONV_WIDTH ** -0.5),
        "conv_dw_b": nrm(15, (N_CONV_LAYERS, CONV_CH), 0.02),
        "conv_ln_g": 1.0 + nrm(16, (N_CONV_LAYERS, CONV_CH), 0.02),
        "conv_ln_b": nrm(17, (N_CONV_LAYERS, CONV_CH), 0.02),
        "sb_w_in": nrm(18, (N_SB_LAYERS, D_MODEL, 3 * SB_WIDTH + MEM_WIDTH), D_MODEL ** -0.5),
        "sb_bias": sb_bias,
        "w_mem_k": nrm(19, (DEPTH, D_MODEL, MEM_WIDTH), D_MODEL ** -0.5),
        "w_mem_v": nrm(20, (DEPTH, D_MODEL, MEM_WIDTH), D_MODEL ** -0.5),
        "w_out": nrm(21, (DEPTH, MIX_OUT, D_MODEL), MIX_OUT ** -0.5),
        "ffn_w_up": nrm(22, (DEPTH, D_MODEL, 2 * D_FF), D_MODEL ** -0.5),
        "ffn_dw_w": nrm(23, (DEPTH, FFN_CONV_WIDTH, 2 * D_FF), FFN_CONV_WIDTH ** -0.5),
        "ffn_dw_b": nrm(24, (DEPTH, 2 * D_FF), 0.02),
        "ffn_w_down": nrm(25, (DEPTH, D_FF, D_MODEL), D_FF ** -0.5),
    }


def reference(x_prompt, x_sample, cache_conv, cache_sb_k, cache_sb_v, cache_mem_k,
              cache_mem_v, state_ffn_conv, page_table, mem_prompt,
              norm_mix_g, norm_ffn_g, norm_final_g, conv_w_in, conv_dw_w, conv_dw_b,
              conv_ln_g, conv_ln_b, sb_w_in, sb_bias, w_mem_k, w_mem_v, w_out,
              ffn_w_up, ffn_dw_w, ffn_dw_b, ffn_w_down):

    def trunk(x, q_start, mem_kv, conv_bufs, past_kv, ffn_bufs):
        b, t, _ = x.shape
        new_conv, new_k, new_v, new_ffn = [], [], [], []
        for i in range(DEPTH):
            j = i // N_MIXERS
            h = _rmsnorm(x, norm_mix_g[i])
            if i % N_MIXERS == 0:
                proj = h @ conv_w_in[j]
                val, gate, qm = jnp.split(proj, [CONV_CH, 2 * CONV_CH], axis=-1)
                u = val * jax.nn.sigmoid(gate)
                u_ext = jnp.concatenate([conv_bufs[j].astype(u.dtype), u], axis=1)
                c = _causal_dwconv(u_ext, conv_dw_w[j], conv_dw_b[j])
                mix = jax.nn.silu(_layernorm(c, conv_ln_g[j], conv_ln_b[j]))
                new_conv.append(u_ext[:, -(CONV_WIDTH - 1):])
            else:
                proj = h @ sb_w_in[j]
                q, kk, vv, qm = jnp.split(proj, [SB_WIDTH, 2 * SB_WIDTH, 3 * SB_WIDTH], axis=-1)
                q = q.reshape(b, t, SB_HEADS, HEAD_DIM)
                kk = kk.reshape(b, t, SB_HEADS, HEAD_DIM)
                vv = vv.reshape(b, t, SB_HEADS, HEAD_DIM)
                pk, pv = past_kv[j]
                o = _sb_attention(q, pk + (kk,), pv + (vv,), q_start, sb_bias[j])
                mix = o.reshape(b, t, SB_WIDTH)
                new_k.append(kk)
                new_v.append(vv)
            mk, mv = mem_kv[i]
            mo = _mem_attention(qm.reshape(b, t, MEM_HEADS, HEAD_DIM), mk, mv).reshape(b, t, MEM_WIDTH)
            x = x + jnp.concatenate([mix, mo], axis=-1) @ w_out[i]
            h = _rmsnorm(x, norm_ffn_g[i])
            up = h @ ffn_w_up[i]
            up_ext = jnp.concatenate([ffn_bufs[i].astype(up.dtype), up], axis=1)
            c = _causal_dwconv(up_ext, ffn_dw_w[i], ffn_dw_b[i])
            a, g = jnp.split(c, 2, axis=-1)
            x = x + (a * jax.nn.silu(g)) @ ffn_w_down[i]
            new_ffn.append(up_ext[:, -(FFN_CONV_WIDTH - 1):])
        return _rmsnorm(x, norm_final_g), new_conv, new_k, new_v, new_ffn

    bp = x_prompt.shape[0]
    dt = x_prompt.dtype
    mem_kv_p = [((mem_prompt @ w_mem_k[i]).reshape(bp, MEM_LEN, MEM_HEADS, HEAD_DIM),
                 (mem_prompt @ w_mem_v[i]).reshape(bp, MEM_LEN, MEM_HEADS, HEAD_DIM))
                for i in range(DEPTH)]
    conv0 = [jnp.zeros((bp, CONV_WIDTH - 1, CONV_CH), dt) for _ in range(N_CONV_LAYERS)]
    ffn0 = [jnp.zeros((bp, FFN_CONV_WIDTH - 1, 2 * D_FF), dt) for _ in range(DEPTH)]
    past0 = [((), ()) for _ in range(N_SB_LAYERS)]
    y_prompt, nc_p, nk_p, nv_p, nf_p = trunk(x_prompt, 0, mem_kv_p, conv0, past0, ffn0)

    ds = x_sample.shape[0]
    past_len = page_table.shape[1] * PAGE_SIZE
    mem_kv_s = [(cache_mem_k[i], cache_mem_v[i]) for i in range(DEPTH)]
    conv_s = [cache_conv[j] for j in range(N_CONV_LAYERS)]
    ffn_s = [state_ffn_conv[i] for i in range(DEPTH)]
    past_s = [((cache_sb_k[page_table, j].reshape(ds, past_len, SB_HEADS, HEAD_DIM),),
               (cache_sb_v[page_table, j].reshape(ds, past_len, SB_HEADS, HEAD_DIM),))
              for j in range(N_SB_LAYERS)]
    y_sample, nc_s, nk_s, nv_s, nf_s = trunk(x_sample, past_len, mem_kv_s, conv_s, past_s, ffn_s)

    conv_state_prompt = jnp.stack(nc_p, axis=0)
    conv_state_sample = jnp.stack(nc_s, axis=0)
    sb_k_prompt = jnp.stack(nk_p, axis=1)
    sb_v_prompt = jnp.stack(nv_p, axis=1)
    sb_k_sample = jnp.stack(nk_s, axis=1)
    sb_v_sample = jnp.stack(nv_s, axis=1)
    mem_k_prompt = jnp.stack([m[0] for m in mem_kv_p], axis=0)
    mem_v_prompt = jnp.stack([m[1] for m in mem_kv_p], axis=0)
    ffn_state_prompt = jnp.stack(nf_p, axis=0)
    ffn_state_sample = jnp.stack(nf_s, axis=0)
    return (y_prompt, y_sample, conv_state_prompt, conv_state_sample,
            sb_k_prompt, sb_v_prompt, sb_k_sample, sb_v_sample,
            mem_k_prompt, mem_v_prompt, ffn_state_prompt, ffn_state_sample)
```

```python
import functools

import jax
import jax.numpy as jnp
from jax import lax
from jax.experimental import pallas as pl
from jax.experimental.pallas import tpu as pltpu

D_MODEL = 1024
N_MIXERS = 2
CONV_CH = D_MODEL
CONV_WIDTH = 31
HEAD_DIM = 128
SB_HEADS = 8
SB_WIDTH = SB_HEADS * HEAD_DIM
MEM_HEADS = 4
MEM_WIDTH = MEM_HEADS * HEAD_DIM
D_FF = 2816
FFN_CONV_WIDTH = 3
PAGE_SIZE = 128
ATTN_SCALE = HEAD_DIM ** -0.5
RMS_EPS = 1e-6
LN_EPS = 1e-5

LANES = 128
SUBLANES = 8
VMEM_LIMIT_BYTES = 56 * 1024 * 1024

F32 = jnp.float32
BF16 = jnp.bfloat16


def _params(*sem):
    return pltpu.CompilerParams(dimension_semantics=sem, vmem_limit_bytes=VMEM_LIMIT_BYTES)


def _const_spec(shape):
    nd = len(shape)
    return pl.BlockSpec(shape, lambda *_: (0,) * nd, pipeline_mode=pl.Buffered(1))


def _rms(x, g):
    return x * lax.rsqrt(jnp.mean(x * x, axis=-1, keepdims=True) + RMS_EPS) * g


def _dot(a, b):
    return jnp.dot(a, b, preferred_element_type=F32)


def _dot_nt(a, b):
    return lax.dot_general(a, b, (((1,), (1,)), ((), ())), preferred_element_type=F32)


def _softplus(z):
    return jnp.maximum(z, 0.0) + jnp.log1p(jnp.exp(-jnp.abs(z)))


def _normproj_kernel(x_ref, g_ref, w_ref, *out_refs, norm, glu, outs):
    x = x_ref[...]
    h = _rms(x, g_ref[...]) if norm else x
    hb = h.astype(BF16)
    cache = {}

    def proj(c0, width):
        if (c0, width) not in cache:
            cache[(c0, width)] = _dot(hb, w_ref[:, c0:c0 + width])
        return cache[(c0, width)]

    refs = list(out_refs)
    if glu:
        val = proj(0, CONV_CH)
        gate = proj(CONV_CH, CONV_CH)
        refs.pop(0)[...] = val * jax.nn.sigmoid(gate)
    for (c0, width, _), ref in zip(outs, refs):
        ref[...] = proj(c0, width).astype(ref.dtype)


def _normproj(x, g, w_bf, outs, *, norm=True, glu=False, tm=512):
    m, k = x.shape
    tm = min(tm, m)
    n = w_bf.shape[1]
    out_shapes, out_specs = [], []
    if glu:
        out_shapes.append(jax.ShapeDtypeStruct((m, CONV_CH), F32))
        out_specs.append(pl.BlockSpec((tm, CONV_CH), lambda i: (i, 0)))
    for (_, width, dt) in outs:
        out_shapes.append(jax.ShapeDtypeStruct((m, width), dt))
        out_specs.append(pl.BlockSpec((tm, width), lambda i: (i, 0)))
    return pl.pallas_call(
        functools.partial(_normproj_kernel, norm=norm, glu=glu, outs=tuple(outs)),
        grid=(m // tm,),
        in_specs=[pl.BlockSpec((tm, k), lambda i: (i, 0)),
                  _const_spec((1, k)),
                  _const_spec((k, n))],
        out_specs=out_specs,
        out_shape=out_shapes,
        compiler_params=_params("arbitrary"),
        name="normproj",
    )(x, g.reshape(1, k), w_bf)


CONV_PAD = 32
CONV_CARRY = CONV_WIDTH - 1


def _conv0_kernel(u_ref, cache_ref, dw_ref, db_ref, lg_ref, lb_ref, mix_ref, st_ref,
                  ext_ref, c_ref, *, nb, L):
    t = pl.program_id(1)
    lo = CONV_PAD - CONV_CARRY

    @pl.when(t == 0)
    def _():
        ext_ref[:, lo:CONV_PAD, :] = cache_ref[...]

    ext_ref[:, CONV_PAD:CONV_PAD + L, :] = u_ref[...]
    R = min(L, 128)
    for c0 in range(0, CONV_CH, LANES):
        for r0 in range(0, L, R):
            acc = jnp.broadcast_to(db_ref[:, c0:c0 + LANES], (nb, R, LANES))
            for w in range(CONV_WIDTH):
                s = r0 + lo + w
                acc = acc + ext_ref[:, s:s + R, c0:c0 + LANES] * dw_ref[w:w + 1, c0:c0 + LANES]
            c_ref[:, r0:r0 + R, c0:c0 + LANES] = acc
    c = c_ref[...]
    mu = jnp.mean(c, axis=-1, keepdims=True)
    xc = c - mu
    y = xc * lax.rsqrt(jnp.mean(xc * xc, axis=-1, keepdims=True) + LN_EPS)
    y = y * lg_ref[...] + lb_ref[...]
    mix_ref[...] = (y * jax.nn.sigmoid(y)).astype(mix_ref.dtype)
    tail = ext_ref[:, L + lo:L + CONV_PAD, :]
    st_ref[...] = tail
    ext_ref[:, lo:CONV_PAD, :] = tail


def _conv0(u, cache, dw, db, lg, lb, *, nb, L, mix_dtype):
    b, s, c = u.shape
    grid = (b // nb, s // L)
    return pl.pallas_call(
        functools.partial(_conv0_kernel, nb=nb, L=L),
        grid=grid,
        in_specs=[pl.BlockSpec((nb, L, c), lambda i, t: (i, t, 0)),
                  pl.BlockSpec((nb, CONV_CARRY, c), lambda i, t: (i, 0, 0)),
                  _const_spec((CONV_WIDTH, c)),
                  _const_spec((1, c)), _const_spec((1, c)), _const_spec((1, c))],
        out_specs=[pl.BlockSpec((nb, L, c), lambda i, t: (i, t, 0)),
                   pl.BlockSpec((nb, CONV_CARRY, c), lambda i, t: (i, 0, 0))],
        out_shape=[jax.ShapeDtypeStruct((b, s, c), mix_dtype),
                   jax.ShapeDtypeStruct((b, CONV_CARRY, c), F32)],
        scratch_shapes=[pltpu.VMEM((nb, CONV_PAD + L, c), F32),
                        pltpu.VMEM((nb, L, c), F32)],
        compiler_params=_params("arbitrary", "arbitrary"),
        name="conv0",
    )(u, cache, dw, db.reshape(1, c), lg.reshape(1, c), lb.reshape(1, c))


def _memattn_kernel(q_ref, mk_ref, mv_ref, o_ref):
    for h in range(MEM_HEADS):
        cs = slice(h * HEAD_DIM, (h + 1) * HEAD_DIM)
        qh = q_ref[0, :, cs].astype(BF16)
        s = _dot_nt(qh, mk_ref[0, :, cs]) * ATTN_SCALE
        e = jnp.exp(s - jnp.max(s, axis=-1, keepdims=True))
        p = e / jnp.sum(e, axis=-1, keepdims=True)
        o_ref[0, :, cs] = _dot(p.astype(BF16), mv_ref[0, :, cs]).astype(o_ref.dtype)


def _memattn(qm, mk_bf, mv_bf, *, t, out_dtype):
    b, s, w = qm.shape
    ml = mk_bf.shape[1]
    return pl.pallas_call(
        _memattn_kernel,
        grid=(b, s // t),
        in_specs=[pl.BlockSpec((1, t, w), lambda i, j: (i, j, 0)),
                  pl.BlockSpec((1, ml, w), lambda i, j: (i, 0, 0)),
                  pl.BlockSpec((1, ml, w), lambda i, j: (i, 0, 0))],
        out_specs=pl.BlockSpec((1, t, w), lambda i, j: (i, j, 0)),
        out_shape=jax.ShapeDtypeStruct((b, s, w), out_dtype),
        compiler_params=_params("arbitrary", "arbitrary"),
        name="memattn",
    )(qm, mk_bf, mv_bf)


FFN_PAD = 8
FFN_CARRY = FFN_CONV_WIDTH - 1
FFN_CHUNK = 256


def _outffn_kernel(x_ref, mix_ref, mo_ref, wo_ref, g_ref, wup_ref, dw_ref, db_ref, wdn_ref,
                   st_ref, gf_ref, y_ref, sto_ref, ext_ref, act_ref, *, nb, L, final):
    t = pl.program_id(1)
    m = nb * L
    lo = FFN_PAD - FFN_CARRY
    x = x_ref[...].reshape(m, D_MODEL)
    mix = mix_ref[...].reshape(m, D_MODEL).astype(BF16)
    mo = mo_ref[...].reshape(m, MEM_WIDTH).astype(BF16)
    x1 = x + _dot(mix, wo_ref[0:D_MODEL, :]) + _dot(mo, wo_ref[D_MODEL:D_MODEL + MEM_WIDTH, :])
    hb = _rms(x1, g_ref[...]).astype(BF16)

    @pl.when(t == 0)
    def _():
        ext_ref[:, lo:FFN_PAD, :] = st_ref[...]

    for j in range(2 * D_FF // FFN_CHUNK):
        cs = slice(j * FFN_CHUNK, (j + 1) * FFN_CHUNK)
        ext_ref[:, FFN_PAD:FFN_PAD + L, cs] = _dot(hb, wup_ref[:, cs]).reshape(nb, L, FFN_CHUNK)

    def conv(cs):
        acc = db_ref[:, cs]
        for w in range(FFN_CONV_WIDTH):
            acc = acc + ext_ref[:, lo + w:lo + w + L, cs] * dw_ref[w:w + 1, cs]
        return acc

    for j in range(D_FF // FFN_CHUNK):
        ca = conv(slice(j * FFN_CHUNK, (j + 1) * FFN_CHUNK))
        cg = conv(slice(D_FF + j * FFN_CHUNK, D_FF + (j + 1) * FFN_CHUNK))
        act = ca * (cg * jax.nn.sigmoid(cg))
        act_ref[:, j * FFN_CHUNK:(j + 1) * FFN_CHUNK] = act.reshape(m, FFN_CHUNK).astype(BF16)

    tail = ext_ref[:, L + lo:L + FFN_PAD, :]
    sto_ref[...] = tail
    ext_ref[:, lo:FFN_PAD, :] = tail
    y = x1 + _dot(act_ref[...], wdn_ref[...])
    if final:
        y = _rms(y, gf_ref[...])
    y_ref[...] = y.reshape(nb, L, D_MODEL)


def _outffn(x, mix, mo, wo_bf, g, wup_bf, dw, db, wdn_bf, state, gf, *, nb, L, final):
    b, s, d = x.shape
    f2 = 2 * D_FF
    row = lambda w: pl.BlockSpec((nb, L, w), lambda i, t: (i, t, 0))
    st = pl.BlockSpec((nb, FFN_CARRY, f2), lambda i, t: (i, 0, 0))
    return pl.pallas_call(
        functools.partial(_outffn_kernel, nb=nb, L=L, final=final),
        grid=(b // nb, s // L),
        in_specs=[row(d), row(D_MODEL), row(MEM_WIDTH),
                  _const_spec(wo_bf.shape), _const_spec((1, d)), _const_spec(wup_bf.shape),
                  _const_spec((FFN_CONV_WIDTH, f2)), _const_spec((1, f2)),
                  _const_spec(wdn_bf.shape), st, _const_spec((1, d))],
        out_specs=[row(d), st],
        out_shape=[jax.ShapeDtypeStruct((b, s, d), F32),
                   jax.ShapeDtypeStruct((b, FFN_CARRY, f2), F32)],
        scratch_shapes=[pltpu.VMEM((nb, FFN_PAD + L, f2), F32),
                        pltpu.VMEM((nb * L, D_FF), BF16)],
        compiler_params=_params("arbitrary", "arbitrary"),
        name="outffn",
    )(x, mix, mo, wo_bf, g.reshape(1, d), wup_bf, dw, db.reshape(1, f2), wdn_bf, state,
      gf.reshape(1, d))


def _later_rows(n):
    j = lax.broadcasted_iota(jnp.int32, (2 * n, n), 0) & (n - 1)
    s = lax.broadcasted_iota(jnp.int32, (2 * n, n), 1)
    return (j > s).astype(BF16)


def _later_cols(n):
    s = lax.broadcasted_iota(jnp.int32, (n, 2 * n), 0)
    j = lax.broadcasted_iota(jnp.int32, (n, 2 * n), 1) & (n - 1)
    return (j > s).astype(BF16)


def _split_hi_lo(x, axis):
    hi = x.astype(BF16)
    lo = (x - hi.astype(F32)).astype(BF16)
    return jnp.concatenate([hi, lo], axis=axis)


def _sb_prompt_kernel(bias_ref, q_ref, k_ref, v_ref, o_ref, *, tq, tk):
    h = pl.program_id(1)
    qi = pl.program_id(2)
    bias = bias_ref[h]
    q = q_ref[0]
    u2 = _later_rows(tk)
    nd = tq // tk

    def block(kb, carry, acc, masked):
        start = pl.multiple_of(kb * tk, tk)
        k = k_ref[0, pl.ds(start, tk), :]
        v = v_ref[0, pl.ds(start, tk), :]
        z = _dot_nt(q, k) * ATTN_SCALE + bias
        sp = _softplus(z)
        logb = z - sp
        if masked:
            qpos = qi * tq + lax.broadcasted_iota(jnp.int32, (tq, tk), 0)
            kpos = start + lax.broadcasted_iota(jnp.int32, (tq, tk), 1)
            msk = kpos < qpos
            sp = jnp.where(msk, sp, 0.0)
        later = _dot(_split_hi_lo(sp, 1), u2)
        a = jnp.exp(logb - later - carry)
        if masked:
            a = jnp.where(msk, a, 0.0)
        acc = acc + _dot(a.astype(BF16), v)
        carry = carry + jnp.sum(sp, axis=1, keepdims=True)
        return carry, acc

    carry = jnp.zeros((tq, 1), F32)
    acc = jnp.zeros((tq, HEAD_DIM), F32)
    for n in range(nd):
        carry, acc = block(qi * nd + (nd - 1 - n), carry, acc, True)
    nfull = qi * nd
    carry, acc = lax.fori_loop(
        0, nfull, lambda n, ca: block(nfull - 1 - n, ca[0], ca[1], False), (carry, acc))
    o_ref[0] = acc.astype(o_ref.dtype)


def _sb_prompt(q_bf, k_bf, v_bf, bias, *, tq=512, tk=256):
    b, s, _ = q_bf.shape
    return pl.pallas_call(
        functools.partial(_sb_prompt_kernel, tq=tq, tk=tk),
        grid=(b, SB_HEADS, s // tq),
        in_specs=[pl.BlockSpec(memory_space=pltpu.SMEM),
                  pl.BlockSpec((1, tq, HEAD_DIM), lambda i, h, j: (i, j, h)),
                  pl.BlockSpec((1, s, HEAD_DIM), lambda i, h, j: (i, 0, h)),
                  pl.BlockSpec((1, s, HEAD_DIM), lambda i, h, j: (i, 0, h))],
        out_specs=pl.BlockSpec((1, tq, HEAD_DIM), lambda i, h, j: (i, j, h)),
        out_shape=jax.ShapeDtypeStruct(q_bf.shape, BF16),
        compiler_params=_params("arbitrary", "arbitrary", "arbitrary"),
        name="sb_prompt",
    )(bias, q_bf, k_bf, v_bf)


SBS_PAGES = 4
SBS_ROWS = 128


def _sb_sample_kernel(pt_ref, q_ref, kn_ref, vn_ref, bias_ref, *refs, tq):
    del pt_ref
    k_refs = refs[:SBS_PAGES]
    v_refs = refs[SBS_PAGES:2 * SBS_PAGES]
    o_ref = refs[2 * SBS_PAGES]
    acc_ref, carry_ref, qbd_ref = refs[2 * SBS_PAGES + 1:]
    p = pl.program_id(1)
    u2t = _later_cols(PAGE_SIZE)
    bias = bias_ref[...]

    def chunk(k, v, masked):
        z = _dot_nt(k.astype(BF16), qbd_ref[...]) * ATTN_SCALE + bias
        sp = _softplus(z)
        logb = z - sp
        if masked:
            key = lax.broadcasted_iota(jnp.int32, z.shape, 0)
            qry = lax.broadcasted_iota(jnp.int32, z.shape, 1) & (tq - 1)
            msk = key < qry
            sp = jnp.where(msk, sp, 0.0)
        later = _dot(u2t, _split_hi_lo(sp, 0))
        a = jnp.exp(logb - later - carry_ref[0:1, :])
        if masked:
            a = jnp.where(msk, a, 0.0)
        acc_ref[...] += _dot(a.T.astype(BF16), v.astype(BF16))
        carry_ref[0:1, :] += jnp.sum(sp, axis=0, keepdims=True)

    @pl.when(p == 0)
    def _():
        q = q_ref[0]
        qt = jnp.concatenate([q] * (SBS_ROWS // tq), axis=0)
        rh = lax.broadcasted_iota(jnp.int32, qt.shape, 0) >> (tq.bit_length() - 1)
        ch = lax.broadcasted_iota(jnp.int32, qt.shape, 1) >> (HEAD_DIM.bit_length() - 1)
        qbd_ref[...] = jnp.where(rh == ch, qt, 0.0).astype(BF16)
        acc_ref[...] = jnp.zeros_like(acc_ref)
        carry_ref[...] = jnp.zeros_like(carry_ref)
        pad = jnp.zeros((PAGE_SIZE - tq, SB_WIDTH), F32)
        chunk(jnp.concatenate([kn_ref[0], pad], axis=0),
              jnp.concatenate([vn_ref[0], pad], axis=0), True)

    def page_rows(ref):
        return jnp.concatenate(
            [ref[0, pl.ds(h, PAGE_SIZE, stride=SB_HEADS), :] for h in range(SB_HEADS)], axis=1)

    for i in range(SBS_PAGES):
        chunk(page_rows(k_refs[i]), page_rows(v_refs[i]), False)

    @pl.when(p == pl.num_programs(1) - 1)
    def _():
        for h in range(SB_HEADS):
            o_ref[0, :, h * HEAD_DIM:(h + 1) * HEAD_DIM] = (
                acc_ref[h * tq:(h + 1) * tq, h * HEAD_DIM:(h + 1) * HEAD_DIM])


def _sb_sample(q, k_new, v_new, cache_k, cache_v, page_table, bias, layer, n_layers):
    b, tq, w = q.shape
    n_pages = page_table.shape[1]
    steps = n_pages // SBS_PAGES
    bias_row = jnp.zeros((1, SBS_ROWS), F32).at[0, :SB_HEADS * tq].set(jnp.repeat(bias, tq))

    def page_spec(i):
        def imap(bi, p, pt):
            return (pt[bi, n_pages - 1 - (p * SBS_PAGES + i)] * n_layers + layer, 0, 0)
        return pl.BlockSpec((1, PAGE_SIZE * SB_HEADS, HEAD_DIM), imap)

    new = pl.BlockSpec((1, tq, w), lambda bi, p, pt: (bi, 0, 0))
    grid_spec = pltpu.PrefetchScalarGridSpec(
        num_scalar_prefetch=1,
        grid=(b, steps),
        in_specs=[new, new, new, pl.BlockSpec((1, SBS_ROWS), lambda bi, p, pt: (0, 0))]
        + [page_spec(i) for i in range(SBS_PAGES)] * 2,
        out_specs=new,
        scratch_shapes=[pltpu.VMEM((SBS_ROWS, w), F32),
                        pltpu.VMEM((SUBLANES, SBS_ROWS), F32),
                        pltpu.VMEM((SBS_ROWS, w), BF16)],
    )
    return pl.pallas_call(
        functools.partial(_sb_sample_kernel, tq=tq),
        grid_spec=grid_spec,
        out_shape=jax.ShapeDtypeStruct((b, tq, w), F32),
        compiler_params=_params("arbitrary", "arbitrary"),
        name="sb_sample",
    )(page_table, q, k_new, v_new, bias_row,
      *([cache_k] * SBS_PAGES), *([cache_v] * SBS_PAGES))


def kernel(x_prompt, x_sample, cache_conv, cache_sb_k, cache_sb_v, cache_mem_k, cache_mem_v,
           state_ffn_conv, page_table, mem_prompt, norm_mix_g, norm_ffn_g, norm_final_g,
           conv_w_in, conv_dw_w, conv_dw_b, conv_ln_g, conv_ln_b, sb_w_in, sb_bias, w_mem_k,
           w_mem_v, w_out, ffn_w_up, ffn_dw_w, ffn_dw_b, ffn_w_down):
    depth = w_out.shape[0]
    n_sb = sb_w_in.shape[0]
    bp, seq, _ = x_prompt.shape
    ds, dseq, _ = x_sample.shape
    mem_len = mem_prompt.shape[1]

    conv_w_in_bf = conv_w_in.astype(BF16)
    sb_w_in_bf = sb_w_in.astype(BF16)
    w_out_bf = w_out.astype(BF16)
    ffn_w_up_bf = ffn_w_up.astype(BF16)
    ffn_w_down_bf = ffn_w_down.astype(BF16)

    w_mem = jnp.concatenate([w for i in range(depth) for w in (w_mem_k[i], w_mem_v[i])],
                            axis=1).astype(BF16)
    mem_outs = []
    for i in range(2 * depth):
        mem_outs += [(i * MEM_WIDTH, MEM_WIDTH, F32), (i * MEM_WIDTH, MEM_WIDTH, BF16)]
    mem_proj = _normproj(mem_prompt.reshape(bp * mem_len, D_MODEL), jnp.ones((D_MODEL,), F32),
                         w_mem, mem_outs, norm=False)
    mem_f32 = [a.reshape(bp, mem_len, MEM_WIDTH) for a in mem_proj[0::2]]
    mem_bf = [a.reshape(bp, mem_len, MEM_WIDTH) for a in mem_proj[1::2]]
    mem_kv_p = [(mem_bf[2 * i], mem_bf[2 * i + 1]) for i in range(depth)]
    mem_kv_s = [(cache_mem_k[i].reshape(ds, mem_len, MEM_WIDTH).astype(BF16),
                 cache_mem_v[i].reshape(ds, mem_len, MEM_WIDTH).astype(BF16))
                for i in range(depth)]

    pool = cache_sb_k.shape[0]
    cache_k2 = cache_sb_k.reshape(pool * n_sb, PAGE_SIZE * SB_HEADS, HEAD_DIM)
    cache_v2 = cache_sb_v.reshape(pool * n_sb, PAGE_SIZE * SB_HEADS, HEAD_DIM)

    def trunk(x, sample, mem_kv, conv_bufs, ffn_bufs):
        b, t, _ = x.shape
        m = b * t
        if sample:
            nb, L, tmem, act_dt, tm = b, t, t, F32, m
        else:
            nb, L, tmem, act_dt, tm = 1, 256, 512, BF16, 512
        new_conv, new_k, new_v, new_ffn = [], [], [], []
        for i in range(depth):
            j = i // N_MIXERS
            x2 = x.reshape(m, D_MODEL)
            if i % N_MIXERS == 0:
                u, qm = _normproj(x2, norm_mix_g[i], conv_w_in_bf[j],
                                  [(2 * CONV_CH, MEM_WIDTH, act_dt)], glu=True, tm=tm)
                mix, st = _conv0(u.reshape(b, t, CONV_CH), conv_bufs[j], conv_dw_w[j],
                                 conv_dw_b[j], conv_ln_g[j], conv_ln_b[j],
                                 nb=nb, L=L, mix_dtype=act_dt)
                new_conv.append(st)
            else:
                q_dt = F32 if sample else BF16
                q, kk, vv, kb, vb, qm = _normproj(
                    x2, norm_mix_g[i], sb_w_in_bf[j],
                    [(0, SB_WIDTH, q_dt), (SB_WIDTH, SB_WIDTH, F32),
                     (2 * SB_WIDTH, SB_WIDTH, F32), (SB_WIDTH, SB_WIDTH, BF16),
                     (2 * SB_WIDTH, SB_WIDTH, BF16), (3 * SB_WIDTH, MEM_WIDTH, act_dt)], tm=tm)
                kk = kk.reshape(b, t, SB_WIDTH)
                vv = vv.reshape(b, t, SB_WIDTH)
                if sample:
                    mix = _sb_sample(q.reshape(b, t, SB_WIDTH), kk, vv, cache_k2, cache_v2,
                                     page_table, sb_bias[j], j, n_sb)
                else:
                    mix = _sb_prompt(q.reshape(b, t, SB_WIDTH), kb.reshape(b, t, SB_WIDTH),
                                     vb.reshape(b, t, SB_WIDTH), sb_bias[j])
                new_k.append(kk.reshape(b, t, SB_HEADS, HEAD_DIM))
                new_v.append(vv.reshape(b, t, SB_HEADS, HEAD_DIM))
            mo = _memattn(qm.reshape(b, t, MEM_WIDTH), mem_kv[i][0], mem_kv[i][1],
                          t=tmem, out_dtype=act_dt)
            x, fst = _outffn(x, mix, mo, w_out_bf[i], norm_ffn_g[i], ffn_w_up_bf[i],
                             ffn_dw_w[i], ffn_dw_b[i], ffn_w_down_bf[i], ffn_bufs[i],
                             norm_final_g, nb=nb, L=L, final=(i == depth - 1))
            new_ffn.append(fst)
        return x, new_conv, new_k, new_v, new_ffn

    n_conv = conv_w_in.shape[0]
    conv0 = [jnp.zeros((bp, CONV_CARRY, CONV_CH), F32) for _ in range(n_conv)]
    ffn0 = [jnp.zeros((bp, FFN_CARRY, 2 * D_FF), F32) for _ in range(depth)]
    y_p, nc_p, nk_p, nv_p, nf_p = trunk(x_prompt, False, mem_kv_p, conv0, ffn0)
    y_s, nc_s, nk_s, nv_s, nf_s = trunk(
        x_sample, True, mem_kv_s, [cache_conv[j] for j in range(n_conv)],
        [state_ffn_conv[i] for i in range(depth)])

    mem4 = lambda a: a.reshape(bp, mem_len, MEM_HEADS, HEAD_DIM)
    return (y_p, y_s,
            jnp.stack(nc_p, axis=0), jnp.stack(nc_s, axis=0),
            jnp.stack(nk_p, axis=1), jnp.stack(nv_p, axis=1),
            jnp.stack(nk_s, axis=1), jnp.stack(nv_s, axis=1),
            jnp.stack([mem4(mem_f32[2 * i]) for i in range(depth)], axis=0),
            jnp.stack([mem4(mem_f32[2 * i + 1]) for i in range(depth)], axis=0),
            jnp.stack(nf_p, axis=0), jnp.stack(nf_s, axis=0))
```
